```python
import jax
import jax.numpy as jnp
from jax import lax
import numpy as np

D_MODEL = 2048
BATCH = 1
SEQ = 8192
DEPTH = 2

M_HEADS = 4
M_QK = 256
M_V = 512
M_CHUNK = 128
A_HEADS = 8
A_QK = 128
A_V = 2 * A_QK
Q_BLOCK = 128
ROPE_THETA = 500000.0
ROPE_DIM = A_QK // 4
N_EXPERTS = 32
TOP_K = 4
D_FF = D_MODEL
SWIGLU_LIMIT = 7.0
SWIGLU_ALPHA = 1.702
E_BLOCK = 256
EPS = 1e-6

M_QW = M_HEADS * M_QK
M_VW = M_HEADS * M_V
M_NG = 4 * M_HEADS
A_QW = A_HEADS * 2 * A_QK
A_VW = A_HEADS * A_V
IN_SIZES = (M_QW, M_QW, M_VW, M_VW, M_NG, A_QW, A_QW, A_VW, D_MODEL, D_MODEL)
IN_COLS = 2 * M_QW + 2 * M_VW + M_NG + 2 * A_QW + A_VW + 2 * D_MODEL

kernel_name = "hybrid_mlstm_diffattn_moe_encoder"


def _rmsnorm(x, w):
    x32 = x.astype(jnp.float32)
    y = x32 * lax.rsqrt(jnp.mean(x32 * x32, axis=-1, keepdims=True) + EPS)
    return (y * w.astype(jnp.float32)).astype(x.dtype)


def _modulate(h, shift, scale):
    return h * (1 + scale[:, None, :]) + shift[:, None, :]


def _rope_tables(S):
    inv = ROPE_THETA ** (-jnp.arange(0, ROPE_DIM, 2, dtype=jnp.float32) / ROPE_DIM)
    ang = jnp.arange(S, dtype=jnp.float32)[:, None] * inv[None, :]
    return jnp.cos(ang), jnp.sin(ang)


def _rope_partial(t, cos, sin):
    half = ROPE_DIM // 2
    c = cos[None, :, None, None, :]
    s = sin[None, :, None, None, :]
    t32 = t.astype(jnp.float32)
    x1 = t32[..., :half]
    x2 = t32[..., half:ROPE_DIM]
    out = jnp.concatenate([x1 * c - x2 * s, x1 * s + x2 * c, t32[..., ROPE_DIM:]], axis=-1)
    return out.astype(t.dtype)


def _mlstm_scan(q, k, v, i_pre, log_f):
    B, H, S, DK = q.shape
    DV = v.shape[-1]
    nc = S // M_CHUNK

    def to_chunks(t):
        return jnp.moveaxis(t.reshape(B, H, nc, M_CHUNK, *t.shape[3:]), 2, 0)

    xs = (to_chunks(q), to_chunks(k), to_chunks(v), to_chunks(i_pre), to_chunks(log_f))
    tri = jnp.tril(jnp.ones((M_CHUNK, M_CHUNK), dtype=bool))

    def step(carry, inp):
        C, n, m = carry
        qb, kb, vb, ib, fb = inp
        b = jnp.cumsum(fb, axis=-1)
        dmat = jnp.where(tri, b[..., :, None] - b[..., None, :] + ib[..., None, :], -jnp.inf)
        inter = b + m[..., None]
        m_t = jnp.maximum(inter, jnp.max(dmat, axis=-1))
        w_intra = jnp.exp(dmat - m_t[..., None])
        w_inter = jnp.exp(inter - m_t)
        s = jnp.einsum("bhtd,bhsd->bhts", qb, kb) * w_intra
        num = jnp.einsum("bhts,bhsv->bhtv", s, vb) + w_inter[..., None] * jnp.einsum("bhtd,bhdv->bhtv", qb, C)
        den = jnp.sum(s, axis=-1) + w_inter * jnp.einsum("bhtd,bhd->bht", qb, n)
        h = num / jnp.maximum(jnp.abs(den), jnp.exp(-m_t))[..., None]
        b_last = b[..., -1]
        src = b_last[..., None] - b + ib
        m_new = jnp.maximum(b_last + m, jnp.max(src, axis=-1))
        w_src = jnp.exp(src - m_new[..., None])
        decay = jnp.exp(b_last + m - m_new)
        C_new = decay[..., None, None] * C + jnp.einsum("bhsd,bhsv->bhdv", kb * w_src[..., None], vb)
        n_new = decay[..., None] * n + jnp.einsum("bhs,bhsd->bhd", w_src, kb)
        return (C_new, n_new, m_new), h

    init = (jnp.zeros((B, H, DK, DV), jnp.float32),
            jnp.zeros((B, H, DK), jnp.float32),
            jnp.zeros((B, H), jnp.float32))
    _, h = lax.scan(step, init, xs)
    return jnp.moveaxis(h, 0, 2).reshape(B, H, S, DV)


def _flip_seq(t):
    return jnp.flip(t, axis=2)


def _mlstm_branch(q, k, v, o, g, b_gates, w_norm):
    B, S, _ = q.shape

    def heads(t, d):
        return t.reshape(B, S, M_HEADS, d).transpose(0, 2, 1, 3).astype(jnp.float32)

    qh = heads(q, M_QK) * (M_QK ** -0.5)
    kh = heads(k, M_QK)
    vh = heads(v, M_V)
    gp = (g.astype(jnp.float32) + b_gates.astype(jnp.float32)).reshape(B, S, 4, M_HEADS).transpose(2, 0, 3, 1)
    i_fw, f_fw, i_bw, f_bw = gp[0], gp[1], gp[2], gp[3]
    h_fw = _mlstm_scan(qh, kh, vh, i_fw, jax.nn.log_sigmoid(f_fw))
    h_bw = _flip_seq(_mlstm_scan(_flip_seq(qh), _flip_seq(kh), _flip_seq(vh),
                                 _flip_seq(i_bw), _flip_seq(jax.nn.log_sigmoid(f_bw))))
    h = _rmsnorm(h_fw + h_bw, w_norm[:, None, :])
    h = h.transpose(0, 2, 1, 3).reshape(B, S, M_VW)
    return (h * jax.nn.sigmoid(o.astype(jnp.float32))).astype(q.dtype)


def _diff_attention(q, k, v, lam_q1, lam_k1, lam_q2, lam_k2, w_subln, lam_init, cos, sin):
    B, S, _ = q.shape
    qh = _rope_partial(q.reshape(B, S, A_HEADS, 2, A_QK), cos, sin) * (A_QK ** -0.5)
    kh = _rope_partial(k.reshape(B, S, A_HEADS, 2, A_QK), cos, sin)
    vh = v.reshape(B, S, A_HEADS, A_V)
    lam = (jnp.exp(jnp.sum(lam_q1.astype(jnp.float32) * lam_k1.astype(jnp.float32)))
           - jnp.exp(jnp.sum(lam_q2.astype(jnp.float32) * lam_k2.astype(jnp.float32))) + lam_init)
    nb = S // Q_BLOCK
    qb = jnp.moveaxis(qh.reshape(B, nb, Q_BLOCK, A_HEADS, 2, A_QK), 1, 0)

    def block(qblk):
        s = jnp.einsum("bqhcd,bkhcd->bhcqk", qblk, kh, preferred_element_type=jnp.float32)
        p = jax.nn.softmax(s, axis=-1)
        a = p[:, :, 0] - lam * p[:, :, 1]
        return jnp.einsum("bhqk,bkhv->bqhv", a.astype(vh.dtype), vh)

    o = lax.map(block, qb)
    o = jnp.moveaxis(o, 0, 1).reshape(B, S, A_HEADS, A_V)
    o = _rmsnorm(o, w_subln) * (1.0 - lam_init)
    return o.reshape(B, S, A_VW)


def _mixer(h, w_in, b_mgates, m_norm, lam_q1, lam_k1, lam_q2, lam_k2, a_norm,
           w_br_m, w_br_a, w_out, lam_init, cos, sin):
    proj = h @ w_in
    bounds = []
    acc = 0
    for size in IN_SIZES[:-1]:
        acc += size
        bounds.append(acc)
    mq, mk, mv, mo, mg, aq, ak, av, gm, ga = jnp.split(proj, bounds, axis=-1)
    y_m = _mlstm_branch(mq, mk, mv, mo, mg, b_mgates, m_norm)
    y_a = _diff_attention(aq, ak, av, lam_q1, lam_k1, lam_q2, lam_k2, a_norm, lam_init, cos, sin)
    merged = jax.nn.sigmoid(gm) * (y_m @ w_br_m) + jax.nn.sigmoid(ga) * (y_a @ w_br_a)
    return merged @ w_out


def _moe(h, w_router, b_router, w_gu, b_gu, w_down, b_down):
    B, S, D = h.shape
    N = B * S
    NK = N * TOP_K
    P = -(-NK // E_BLOCK) * E_BLOCK + N_EXPERTS * E_BLOCK
    nblk = P // E_BLOCK
    xt = h.reshape(N, D)
    logits = (xt @ w_router + b_router).astype(jnp.float32)
    top_val, top_idx = lax.top_k(logits, TOP_K)
    gates = jax.nn.softmax(top_val, axis=-1)
    eid = top_idx.reshape(NK)
    tok = jnp.repeat(jnp.arange(N, dtype=jnp.int32), TOP_K)
    order = jnp.argsort(eid)
    e_sorted = eid[order]
    counts = jnp.bincount(eid, length=N_EXPERTS)
    starts = jnp.cumsum(counts) - counts
    padded = (counts + E_BLOCK - 1) // E_BLOCK * E_BLOCK
    pends = jnp.cumsum(padded)
    pstarts = pends - padded
    dest = pstarts[e_sorted] + jnp.arange(NK, dtype=counts.dtype) - starts[e_sorted]
    slot_tok = jnp.zeros((P,), jnp.int32).at[dest].set(tok[order])
    slot_gate = jnp.zeros((P,), jnp.float32).at[dest].set(gates.reshape(NK)[order])
    blk_e = jnp.minimum(jnp.searchsorted(pends, jnp.arange(nblk, dtype=pends.dtype) * E_BLOCK, side="right"),
                        N_EXPERTS - 1)
    xb = xt[slot_tok].reshape(nblk, E_BLOCK, D)

    def expert_block(args):
        xe, e = args
        gu = xe @ w_gu[e] + b_gu[e]
        g = jnp.minimum(gu[:, :D_FF], SWIGLU_LIMIT)
        u = jnp.clip(gu[:, D_FF:], -SWIGLU_LIMIT, SWIGLU_LIMIT)
        act = (u + 1) * (g * jax.nn.sigmoid(SWIGLU_ALPHA * g))
        return act @ w_down[e] + b_down[e]

    ys = lax.map(expert_block, (xb, blk_e)).reshape(P, D)
    out = jnp.zeros((N, D), ys.dtype).at[slot_tok].add(ys * slot_gate[:, None].astype(ys.dtype))
    return out.reshape(B, S, D)


def setup_inputs(seed: int = 0) -> dict:
    key = jax.random.key(seed)
    ks = jax.random.split(key, 26)
    L, D, E, F = DEPTH, D_MODEL, N_EXPERTS, D_FF

    def nrm(k, shape, scale):
        return jax.random.normal(k, shape, jnp.float32) * scale

    gate_offset = jnp.array([0.0, 3.0, 0.0, 3.0], jnp.float32)[None, :, None]
    return {
        "x": nrm(ks[0], (BATCH, SEQ, D), 1.0),
        "c": nrm(ks[1], (BATCH, D), 1.0),
        "norm_mix": 1.0 + nrm(ks[2], (L, D), 0.02),
        "norm_ffn": 1.0 + nrm(ks[3], (L, D), 0.02),
        "w_ada": nrm(ks[4], (L, D, 6 * D), 0.5 * D ** -0.5),
        "b_ada": nrm(ks[5], (L, 6 * D), 0.02),
        "w_in": nrm(ks[6], (L, D, IN_COLS), D ** -0.5),
        "b_mgates": (gate_offset + nrm(ks[7], (L, 4, M_HEADS), 0.5)).reshape(L, M_NG),
        "m_norm": 1.0 + nrm(ks[8], (L, M_HEADS, M_V), 0.02),
        "lam_q1": nrm(ks[9], (L, A_QK), 0.1),
        "lam_k1": nrm(ks[10], (L, A_QK), 0.1),
        "lam_q2": nrm(ks[11], (L, A_QK), 0.1),
        "lam_k2": nrm(ks[12], (L, A_QK), 0.1),
        "a_norm": 1.0 + nrm(ks[13], (L, A_V), 0.02),
        "w_br_m": nrm(ks[14], (L, M_VW, D), M_VW ** -0.5),
        "w_br_a": nrm(ks[15], (L, A_VW, D), A_VW ** -0.5),
        "w_out": nrm(ks[16], (L, D, D), D ** -0.5),
        "w_router": nrm(ks[17], (L, D, E), D ** -0.5),
        "b_router": nrm(ks[18], (L, E), 0.01),
        "w_gu": nrm(ks[19], (L, E, D, 2 * F), D ** -0.5),
        "b_gu": nrm(ks[20], (L, E, 2 * F), 0.02),
        "w_down": nrm(ks[21], (L, E, F, D), F ** -0.5),
        "b_down": nrm(ks[22], (L, E, D), 0.02),
        "norm_final": 1.0 + nrm(ks[23], (D,), 0.02),
    }


def reference(x, c, norm_mix, norm_ffn, w_ada, b_ada, w_in, b_mgates, m_norm,
              lam_q1, lam_k1, lam_q2, lam_k2, a_norm, w_br_m, w_br_a, w_out,
              w_router, b_router, w_gu, b_gu, w_down, b_down, norm_final):
    S = x.shape[1]
    cos, sin = _rope_tables(S)
    c_act = jax.nn.silu(c)
    for l in range(DEPTH):
        lam_init = 0.8 - 0.6 * float(np.exp(-0.3 * l))
        mod = c_act @ w_ada[l] + b_ada[l]
        sh_m, sc_m, g_m, sh_f, sc_f, g_f = jnp.split(mod, 6, axis=-1)
        h = _modulate(_rmsnorm(x, norm_mix[l]), sh_m, sc_m)
        y = _mixer(h, w_in[l], b_mgates[l], m_norm[l], lam_q1[l], lam_k1[l], lam_q2[l], lam_k2[l],
                   a_norm[l], w_br_m[l], w_br_a[l], w_out[l], lam_init, cos, sin)
        x = x + g_m[:, None, :] * y
        h = _modulate(_rmsnorm(x, norm_ffn[l]), sh_f, sc_f)
        y = _moe(h, w_router[l], b_router[l], w_gu[l], b_gu[l], w_down[l], b_down[l])
        x = x + g_f[:, None, :] * y
    return _rmsnorm(x, norm_final)
```

```python
import functools
import math

import jax
import jax.numpy as jnp
import numpy as np
from jax import lax
from jax.experimental import pallas as pl
from jax.experimental.pallas import tpu as pltpu

F32 = jnp.float32
BF16 = jnp.bfloat16
HIGHEST = lax.Precision.HIGHEST

D_MODEL = 2048
DEPTH = 2
M_HEADS = 4
M_QK = 256
M_V = 512
M_CHUNK = 128
A_HEADS = 8
A_QK = 128
A_V = 2 * A_QK
ROPE_THETA = 500000.0
ROPE_DIM = A_QK // 4
N_EXPERTS = 32
TOP_K = 4
D_FF = D_MODEL
SWIGLU_LIMIT = 7.0
SWIGLU_ALPHA = 1.702
EPS = 1e-6

M_QW = M_HEADS * M_QK
M_VW = M_HEADS * M_V
M_NG = 4 * M_HEADS
A_QW = A_HEADS * 2 * A_QK
A_VW = A_HEADS * A_V
OFF_MG = 2 * M_QW + 2 * M_VW
OFF_A = OFF_MG + M_NG
OFF_GATE = OFF_A + 2 * A_QW + A_VW

LANES = 128
VMEM_LIMIT = 56 * 1024 * 1024
LOG2E = math.log2(math.e)

MOE_SUB = 256
MOE_GROUP = 1024
MOE_TF = 256


def _cparams(sem):
    return pltpu.CompilerParams(dimension_semantics=sem, vmem_limit_bytes=VMEM_LIMIT)


def _ada_kernel(c_ref, w_ref, b_ref, o_ref):
    c = c_ref[...]
    ca = c * jax.nn.sigmoid(c)
    o_ref[...] = jnp.dot(ca, w_ref[...], preferred_element_type=F32, precision=HIGHEST) + b_ref[...]


def _ada(c, w_ada, b_ada):
    L, D, N = w_ada.shape
    tn = 1536
    c8 = jnp.broadcast_to(c.reshape(1, D), (8, D))
    out = pl.pallas_call(
        _ada_kernel,
        grid=(L, N // tn),
        in_specs=[
            pl.BlockSpec((8, D), lambda l, j: (0, 0)),
            pl.BlockSpec((None, D, tn), lambda l, j: (l, 0, j)),
            pl.BlockSpec((None, 1, tn), lambda l, j: (l, 0, j)),
        ],
        out_specs=pl.BlockSpec((None, 8, tn), lambda l, j: (l, 0, j)),
        out_shape=jax.ShapeDtypeStruct((L, 8, N), F32),
        compiler_params=_cparams(("arbitrary", "arbitrary")),
        name="ada",
    )(c8, w_ada, b_ada.reshape(L, 1, N))
    return out[:, 0, :]


def _norm_mod(x, w, sc, sh):
    y = x * lax.rsqrt(jnp.mean(x * x, axis=-1, keepdims=True) + EPS) * w
    return y * (1.0 + sc) + sh


def _prenorm_kernel(x_ref, w_ref, sc_ref, sh_ref, h_ref):
    h_ref[...] = _norm_mod(x_ref[...], w_ref[...], sc_ref[...], sh_ref[...]).astype(BF16)


def _prenorm(x, w, sc, sh, tm=512):
    S, D = x.shape
    vec = pl.BlockSpec((1, D), lambda i: (0, 0))
    return pl.pallas_call(
        _prenorm_kernel,
        grid=(S // tm,),
        in_specs=[pl.BlockSpec((tm, D), lambda i: (i, 0)), vec, vec, vec],
        out_specs=pl.BlockSpec((tm, D), lambda i: (i, 0)),
        out_shape=jax.ShapeDtypeStruct((S, D), BF16),
        compiler_params=_cparams(("arbitrary",)),
        name="prenorm",
    )(x, w.reshape(1, D), sc.reshape(1, D), sh.reshape(1, D))


def _topk_softmax(logits):
    lane = lax.broadcasted_iota(jnp.int32, logits.shape, 1)
    v = logits
    idx_out = jnp.zeros(logits.shape, jnp.int32)
    val_out = jnp.full(logits.shape, -jnp.inf, F32)
    for k in range(TOP_K):
        mx = jnp.max(v, axis=-1, keepdims=True)
        idx = jnp.min(jnp.where(v == mx, lane, LANES), axis=-1, keepdims=True)
        idx_out = jnp.where(lane == k, idx, idx_out)
        val_out = jnp.where(lane == k, mx, val_out)
        v = jnp.where(lane == idx, -jnp.inf, v)
    top0 = jnp.max(val_out, axis=-1, keepdims=True)
    e = jnp.exp(val_out - top0)
    gates = e / jnp.sum(e, axis=-1, keepdims=True)
    return idx_out, gates


def _ffn_prenorm_kernel(x_ref, w_ref, sc_ref, sh_ref, wr_ref, br_ref, h_ref, idx_ref, gate_ref):
    h = _norm_mod(x_ref[...], w_ref[...], sc_ref[...], sh_ref[...])
    h_ref[...] = h.astype(BF16)
    logits = jnp.dot(h, wr_ref[...], preferred_element_type=F32, precision=HIGHEST) + br_ref[...]
    idx, gates = _topk_softmax(logits)
    idx_ref[...] = idx
    gate_ref[...] = gates


def _ffn_prenorm(x, w, sc, sh, w_router, b_router, tm=512):
    S, D = x.shape
    E = w_router.shape[1]
    wr = jnp.zeros((D, LANES), F32).at[:, :E].set(w_router)
    br = jnp.full((1, LANES), -jnp.inf, F32).at[0, :E].set(b_router)
    vec = pl.BlockSpec((1, D), lambda i: (0, 0))
    row = lambda n: pl.BlockSpec((tm, n), lambda i: (i, 0))
    h, idx, gates = pl.pallas_call(
        _ffn_prenorm_kernel,
        grid=(S // tm,),
        in_specs=[row(D), vec, vec, vec,
                  pl.BlockSpec((D, LANES), lambda i: (0, 0)),
                  pl.BlockSpec((1, LANES), lambda i: (0, 0))],
        out_specs=[row(D), row(LANES), row(LANES)],
        out_shape=[jax.ShapeDtypeStruct((S, D), BF16),
                   jax.ShapeDtypeStruct((S, LANES), jnp.int32),
                   jax.ShapeDtypeStruct((S, LANES), F32)],
        compiler_params=_cparams(("arbitrary",)),
        name="ffn_prenorm_router",
    )(x, w.reshape(1, D), sc.reshape(1, D), sh.reshape(1, D), wr, br)
    return h, idx[:, :TOP_K], gates[:, :TOP_K]


def _final_norm_kernel(x_ref, w_ref, o_ref):
    x = x_ref[...]
    o_ref[...] = x * lax.rsqrt(jnp.mean(x * x, axis=-1, keepdims=True) + EPS) * w_ref[...]


def _final_norm(x, w, tm=512):
    S, D = x.shape
    return pl.pallas_call(
        _final_norm_kernel,
        grid=(S // tm,),
        in_specs=[pl.BlockSpec((tm, D), lambda i: (i, 0)), pl.BlockSpec((1, D), lambda i: (0, 0))],
        out_specs=pl.BlockSpec((tm, D), lambda i: (i, 0)),
        out_shape=jax.ShapeDtypeStruct((S, D), F32),
        compiler_params=_cparams(("arbitrary",)),
        name="final_norm",
    )(x, w.reshape(1, D))


def _proj_plain_kernel(x_ref, w_ref, o_ref):
    o_ref[...] = jnp.dot(x_ref[...], w_ref[...], preferred_element_type=F32).astype(o_ref.dtype)


def _proj_plain(x, w, out_dtype, tm=1024, tn=512):
    S, K = x.shape
    N = w.shape[1]
    tn = min(tn, N)
    return pl.pallas_call(
        _proj_plain_kernel,
        grid=(S // tm, N // tn),
        in_specs=[pl.BlockSpec((tm, K), lambda i, j: (i, 0)),
                  pl.BlockSpec((K, tn), lambda i, j: (0, j))],
        out_specs=pl.BlockSpec((tm, tn), lambda i, j: (i, j)),
        out_shape=jax.ShapeDtypeStruct((S, N), out_dtype),
        compiler_params=_cparams(("arbitrary", "arbitrary")),
        name="proj_plain",
    )(x, w)


def _proj_attn_kernel(x_ref, w_ref, c_ref, sa_ref, sb_ref, o_ref, *, n_rope_tiles, n_q_tiles, tn):
    j = pl.program_id(1)
    acc = jnp.dot(x_ref[...], w_ref[...], preferred_element_type=F32)

    @pl.when(j >= n_rope_tiles)
    def _():
        o_ref[...] = acc.astype(o_ref.dtype)

    @pl.when(j < n_rope_tiles)
    def _():
        scale = jnp.where(j < n_q_tiles, A_QK ** -0.5 * LOG2E, 1.0).astype(F32)
        c, sa, sb = c_ref[...], sa_ref[...], sb_ref[...]
        half = ROPE_DIM // 2
        for g in range(tn // LANES):
            t = acc[:, g * LANES:(g + 1) * LANES]
            r = t * c + pltpu.roll(t, half, 1) * sa + pltpu.roll(t, LANES - half, 1) * sb
            o_ref[:, g * LANES:(g + 1) * LANES] = (r * scale).astype(o_ref.dtype)


def _proj_attn(x, w, rope_c, rope_sa, rope_sb, tm=1024, tn=512):
    S, K = x.shape
    N = w.shape[1]
    tab = pl.BlockSpec((tm, LANES), lambda i, j: (i, 0))
    kern = functools.partial(_proj_attn_kernel, n_rope_tiles=2 * A_QW // tn, n_q_tiles=A_QW // tn, tn=tn)
    return pl.pallas_call(
        kern,
        grid=(S // tm, N // tn),
        in_specs=[pl.BlockSpec((tm, K), lambda i, j: (i, 0)),
                  pl.BlockSpec((K, tn), lambda i, j: (0, j)),
                  tab, tab, tab],
        out_specs=pl.BlockSpec((tm, tn), lambda i, j: (i, j)),
        out_shape=jax.ShapeDtypeStruct((S, N), BF16),
        compiler_params=_cparams(("arbitrary", "arbitrary")),
        name="proj_attn",
    )(x, w, rope_c, rope_sa, rope_sb)


def _rope_lane_tables(S):
    half = ROPE_DIM // 2
    inv = ROPE_THETA ** (-jnp.arange(0, ROPE_DIM, 2, dtype=F32) / ROPE_DIM)
    ang = jnp.arange(S, dtype=F32)[:, None] * inv[None, :]
    cos, sin = jnp.cos(ang), jnp.sin(ang)
    ones = jnp.ones((S, LANES - ROPE_DIM), F32)
    zeros_h = jnp.zeros((S, half), F32)
    zeros_r = jnp.zeros((S, LANES - ROPE_DIM), F32)
    c = jnp.concatenate([cos, cos, ones], axis=1)
    sa = jnp.concatenate([zeros_h, sin, zeros_r], axis=1)
    sb = jnp.concatenate([-sin, zeros_h, zeros_r], axis=1)
    return c, sa, sb


def _log_sigmoid(x):
    return jnp.minimum(x, 0.0) - jnp.log(1.0 + jnp.exp(-jnp.abs(x)))


def _mlstm_kernel(q_ref, k_ref, v_ref, gc_ref, gr_ref, h_ref, c_scr, n_scr, m_scr, *, n_chunks):
    hd = pl.program_id(0)
    d = pl.program_id(1)
    j = pl.program_id(2)
    L = M_CHUNK

    @pl.when(j == 0)
    def _():
        c_scr[...] = jnp.zeros_like(c_scr)
        n_scr[...] = jnp.zeros_like(n_scr)
        m_scr[...] = jnp.zeros_like(m_scr)

    row = lax.broadcasted_iota(jnp.int32, (L, L), 0)
    col = lax.broadcasted_iota(jnp.int32, (L, L), 1)
    fwd = d == 0
    dd = jnp.where(fwd, row - col, col - row)
    mask = dd >= 0
    maskf = mask.astype(F32)
    mask_t = (dd <= 0).astype(F32)

    for c in range(n_chunks):
        ci = jnp.where(fwd, c, n_chunks - 1 - c)
        r0 = pl.multiple_of(ci * L, L)
        q = q_ref[pl.ds(r0, L), :] * jnp.asarray(M_QK ** -0.5, BF16)
        k = k_ref[pl.ds(r0, L), :]
        v = v_ref[pl.ds(r0, L), :]
        gcol = gc_ref[pl.ds(r0, L), :]
        grow = gr_ref[:, pl.ds(r0, L)]
        i_col, lf_col = gcol[:, 0:1], _log_sigmoid(gcol[:, 1:2])
        i_row, lf_row = grow[0:1, :], _log_sigmoid(grow[1:2, :])

        b_col = jnp.sum(maskf * lf_row, axis=1, keepdims=True)
        b_row = jnp.sum(mask_t * lf_col, axis=0, keepdims=True)
        total = jnp.sum(lf_row, axis=1, keepdims=True)
        m_prev = m_scr[...]
        n_prev = n_scr[...]
        c_prev = c_scr[...]

        dmat = jnp.where(mask, b_col - b_row + i_row, -jnp.inf)
        inter = b_col + m_prev
        m_t = jnp.maximum(inter, jnp.max(dmat, axis=1, keepdims=True))
        w_intra = jnp.exp(dmat - m_t)
        w_inter = jnp.exp(inter - m_t)
        s = lax.dot_general(q, k, (((1,), (1,)), ((), ())), preferred_element_type=F32) * w_intra
        num = (jnp.dot(s.astype(BF16), v, preferred_element_type=F32)
               + w_inter * jnp.dot(q, c_prev.astype(BF16), preferred_element_type=F32))
        den = (jnp.sum(s, axis=1, keepdims=True)
               + w_inter * jnp.sum(q.astype(F32) * n_prev, axis=1, keepdims=True))
        h_ref[pl.ds(r0, L), :] = num / jnp.maximum(jnp.abs(den), jnp.exp(-m_t))

        src_col = total - b_col + i_col
        src_row = total - b_row + i_row
        m_new = jnp.maximum(total + m_prev, jnp.max(src_row, axis=1, keepdims=True))
        decay = jnp.exp(total + m_prev - m_new)
        kw = k.astype(F32) * jnp.exp(src_col - m_new)
        c_scr[...] = decay * c_prev + lax.dot_general(
            kw.astype(BF16), v, (((0,), (0,)), ((), ())), preferred_element_type=F32)
        n_scr[...] = decay * n_prev + jnp.sum(kw, axis=0, keepdims=True)
        m_scr[...] = m_new


def _mlstm(proj_m, gates, n_chunks=4):
    S = proj_m.shape[0]
    R = n_chunks * M_CHUNK
    nj = S // R
    g = gates.reshape(S, 2, 2, M_HEADS).transpose(1, 3, 0, 2)
    g_col = g
    g_row = g.transpose(0, 1, 3, 2)

    def blk(d, j):
        return j + d * (nj - 1 - 2 * j)

    kern = functools.partial(_mlstm_kernel, n_chunks=n_chunks)
    return pl.pallas_call(
        kern,
        grid=(M_HEADS, 2, nj),
        in_specs=[
            pl.BlockSpec((R, M_QK), lambda h, d, j: (blk(d, j), h)),
            pl.BlockSpec((R, M_QK), lambda h, d, j: (blk(d, j), M_HEADS + h)),
            pl.BlockSpec((R, M_V), lambda h, d, j: (blk(d, j), (2 * M_QW) // M_V + h)),
            pl.BlockSpec((None, None, R, 2), lambda h, d, j: (d, h, blk(d, j), 0)),
            pl.BlockSpec((None, None, 2, R), lambda h, d, j: (d, h, 0, blk(d, j))),
        ],
        out_specs=pl.BlockSpec((None, R, M_V), lambda h, d, j: (d, blk(d, j), h)),
        out_shape=jax.ShapeDtypeStruct((2, S, M_VW), F32),
        scratch_shapes=[pltpu.VMEM((M_QK, M_V), F32), pltpu.VMEM((1, M_QK), F32), pltpu.VMEM((1, 1), F32)],
        compiler_params=_cparams(("arbitrary", "arbitrary", "arbitrary")),
        name="mlstm_scan",
    )(proj_m, proj_m, proj_m, g_col, g_row)


def _mlstm_out_kernel(h_ref, o_ref, w_ref, y_ref):
    hsum = h_ref[0] + h_ref[1]
    for hd in range(M_HEADS):
        sl = slice(hd * M_V, (hd + 1) * M_V)
        x = hsum[:, sl]
        y = x * lax.rsqrt(jnp.mean(x * x, axis=-1, keepdims=True) + EPS) * w_ref[:, sl]
        y_ref[:, sl] = (y * jax.nn.sigmoid(o_ref[:, sl].astype(F32))).astype(y_ref.dtype)


def _mlstm_out(h2, proj_m, m_norm, tm=512):
    S = h2.shape[1]
    o_blk = (2 * M_QW + M_VW) // M_VW
    return pl.pallas_call(
        _mlstm_out_kernel,
        grid=(S // tm,),
        in_specs=[pl.BlockSpec((2, tm, M_VW), lambda i: (0, i, 0)),
                  pl.BlockSpec((tm, M_VW), lambda i: (i, o_blk)),
                  pl.BlockSpec((1, M_VW), lambda i: (0, 0))],
        out_specs=pl.BlockSpec((tm, M_VW), lambda i: (i, 0)),
        out_shape=jax.ShapeDtypeStruct((S, M_VW), BF16),
        compiler_params=_cparams(("arbitrary",)),
        name="mlstm_out",
    )(h2, proj_m, m_norm.reshape(1, M_VW))


def _attn_kernel(lam_ref, q_ref, k_ref, v_ref, w_ref, o_ref, m_scr, l_scr, acc_scr, *, tk, lam_init):
    S = k_ref.shape[0]
    nkv = S // tk
    tq = q_ref.shape[0]
    m_scr[...] = jnp.full(m_scr.shape, -jnp.inf, F32)
    l_scr[...] = jnp.zeros(l_scr.shape, F32)
    acc_scr[...] = jnp.zeros(acc_scr.shape, F32)
    qs = [q_ref[:, c * A_QK:(c + 1) * A_QK] for c in range(2)]

    def body(kb, carry):
        r0 = pl.multiple_of(kb * tk, tk)
        v = v_ref[pl.ds(r0, tk), :]
        for c in range(2):
            k = k_ref[pl.ds(r0, tk), c * A_QK:(c + 1) * A_QK]
            s = lax.dot_general(qs[c], k, (((1,), (1,)), ((), ())), preferred_element_type=F32)
            m_old = m_scr[c]
            m_new = jnp.maximum(m_old, jnp.max(s, axis=1, keepdims=True))
            alpha = jnp.exp2(m_old - m_new)
            p = jnp.exp2(s - m_new)
            l_scr[c] = alpha * l_scr[c] + jnp.sum(p, axis=1, keepdims=True)
            acc_scr[c] = alpha * acc_scr[c] + jnp.dot(p.astype(BF16), v, preferred_element_type=F32)
            m_scr[c] = m_new
        return carry

    lax.fori_loop(0, nkv, body, 0)
    lam = lam_ref[0]
    o = acc_scr[0] / l_scr[0] - lam * (acc_scr[1] / l_scr[1])
    y = o * lax.rsqrt(jnp.mean(o * o, axis=-1, keepdims=True) + EPS) * w_ref[...]
    o_ref[...] = (y * (1.0 - lam_init)).astype(o_ref.dtype)


def _attention(proj_a, lam, a_norm, lam_init, tq=512, tk=1024):
    S = proj_a.shape[0]
    tk = min(tk, S)
    kern = functools.partial(_attn_kernel, tk=tk, lam_init=lam_init)
    return pl.pallas_call(
        kern,
        grid=(A_HEADS, S // tq),
        in_specs=[
            pl.BlockSpec(memory_space=pltpu.SMEM),
            pl.BlockSpec((tq, 2 * A_QK), lambda h, i: (i, h)),
            pl.BlockSpec((S, 2 * A_QK), lambda h, i: (0, A_HEADS + h)),
            pl.BlockSpec((S, A_V), lambda h, i: (0, 2 * A_HEADS + h)),
            pl.BlockSpec((1, A_V), lambda h, i: (0, 0)),
        ],
        out_specs=pl.BlockSpec((tq, A_V), lambda h, i: (i, h)),
        out_shape=jax.ShapeDtypeStruct((S, A_VW), BF16),
        scratch_shapes=[pltpu.VMEM((2, tq, 1), F32), pltpu.VMEM((2, tq, 1), F32),
                        pltpu.VMEM((2, tq, A_V), F32)],
        compiler_params=_cparams(("arbitrary", "arbitrary")),
        name="diff_attention",
    )(lam.reshape(1), proj_a, proj_a, proj_a, a_norm.reshape(1, A_V))


def _merge_kernel(ym_ref, ya_ref, wm_ref, wa_ref, gm_ref, ga_ref, o_ref):
    bm = jnp.dot(ym_ref[...], wm_ref[...], preferred_element_type=F32)
    ba = jnp.dot(ya_ref[...], wa_ref[...], preferred_element_type=F32)
    o_ref[...] = (jax.nn.sigmoid(gm_ref[...].astype(F32)) * bm
                  + jax.nn.sigmoid(ga_ref[...].astype(F32)) * ba).astype(o_ref.dtype)


def _merge(y_m, y_a, w_br_m, w_br_a, proj_g, tm=1024, tn=512):
    S, K = y_m.shape
    N = w_br_m.shape[1]
    xs = pl.BlockSpec((tm, K), lambda i, j: (i, 0))
    ws = pl.BlockSpec((K, tn), lambda i, j: (0, j))
    return pl.pallas_call(
        _merge_kernel,
        grid=(S // tm, N // tn),
        in_specs=[xs, xs, ws, ws,
                  pl.BlockSpec((tm, tn), lambda i, j: (i, j)),
                  pl.BlockSpec((tm, tn), lambda i, j: (i, j + N // tn))],
        out_specs=pl.BlockSpec((tm, tn), lambda i, j: (i, j)),
        out_shape=jax.ShapeDtypeStruct((S, N), BF16),
        compiler_params=_cparams(("arbitrary", "arbitrary")),
        name="merge",
    )(y_m, y_a, w_br_m, w_br_a, proj_g, proj_g)


def _outproj_kernel(m_ref, w_ref, x_ref, g_ref, o_ref):
    y = jnp.dot(m_ref[...], w_ref[...], preferred_element_type=F32)
    o_ref[...] = x_ref[...] + g_ref[...] * y


def _outproj_residual(merged, w_out, x, gate, tm=1024, tn=512):
    S, K = merged.shape
    N = w_out.shape[1]
    return pl.pallas_call(
        _outproj_kernel,
        grid=(S // tm, N // tn),
        in_specs=[pl.BlockSpec((tm, K), lambda i, j: (i, 0)),
                  pl.BlockSpec((K, tn), lambda i, j: (0, j)),
                  pl.BlockSpec((tm, tn), lambda i, j: (i, j)),
                  pl.BlockSpec((1, tn), lambda i, j: (0, j))],
        out_specs=pl.BlockSpec((tm, tn), lambda i, j: (i, j)),
        out_shape=jax.ShapeDtypeStruct((S, N), F32),
        compiler_params=_cparams(("arbitrary", "arbitrary")),
        name="outproj_residual",
    )(merged, w_out, x, gate.reshape(1, N))


def _moe_kernel(ge_ref, gx_ref, ns_ref, x_ref, wg_ref, wu_ref, wd_ref, bg_ref, bu_ref, bd_ref, sg_ref,
                o_ref, wg_s, wu_s, wd_s, *, n_ft):
    g = pl.program_id(0)
    f = pl.program_id(1)
    ns = ns_ref[g]

    @pl.when(f == 0)
    def _():
        o_ref[...] = jnp.zeros_like(o_ref)

    @pl.when(ns > 0)
    def _():
        wg_s[...] = wg_ref[...].astype(BF16)
        wu_s[...] = wu_ref[...].astype(BF16)
        wd_s[...] = wd_ref[...].astype(BF16)

        def sub(r, carry):
            rows = pl.ds(pl.multiple_of(r * MOE_SUB, MOE_SUB), MOE_SUB)
            xr = x_ref[rows, :]
            gg = jnp.dot(xr, wg_s[...], preferred_element_type=F32) + bg_ref[...]
            uu = jnp.dot(xr, wu_s[...], preferred_element_type=F32) + bu_ref[...]
            gg = jnp.minimum(gg, SWIGLU_LIMIT)
            uu = jnp.clip(uu, -SWIGLU_LIMIT, SWIGLU_LIMIT)
            act = (uu + 1.0) * (gg * jax.nn.sigmoid(SWIGLU_ALPHA * gg))
            o_ref[rows, :] += jnp.dot(act.astype(BF16), wd_s[...], preferred_element_type=F32)
            return carry

        lax.fori_loop(0, ns, sub, 0)

    @pl.when(f == n_ft - 1)
    def _():
        o_ref[...] = (o_ref[...] + bd_ref[...]) * sg_ref[...]


def _moe_experts(layer, xg, slot_gate, grp_e, grp_x, grp_ns, w_gu, b_gu, w_down, b_down):
    P, D = xg.shape
    L, E, _, F2 = w_gu.shape
    F = F2 // 2
    G = P // MOE_GROUP
    n_ft = F // MOE_TF
    kern = functools.partial(_moe_kernel, n_ft=n_ft)
    grid_spec = pltpu.PrefetchScalarGridSpec(
        num_scalar_prefetch=3,
        grid=(G, n_ft),
        in_specs=[
            pl.BlockSpec((MOE_GROUP, D), lambda g, f, ge, gx, ns: (gx[g], 0)),
            pl.BlockSpec((None, None, D, MOE_TF), lambda g, f, ge, gx, ns: (layer, ge[g], 0, f)),
            pl.BlockSpec((None, None, D, MOE_TF), lambda g, f, ge, gx, ns: (layer, ge[g], 0, n_ft + f)),
            pl.BlockSpec((None, None, MOE_TF, D), lambda g, f, ge, gx, ns: (layer, ge[g], f, 0)),
            pl.BlockSpec((None, None, 1, MOE_TF), lambda g, f, ge, gx, ns: (layer, ge[g], 0, f)),
            pl.BlockSpec((None, None, 1, MOE_TF), lambda g, f, ge, gx, ns: (layer, ge[g], 0, n_ft + f)),
            pl.BlockSpec((None, None, 1, D), lambda g, f, ge, gx, ns: (layer, ge[g], 0, 0)),
            pl.BlockSpec((MOE_GROUP, 1), lambda g, f, ge, gx, ns: (g, 0)),
        ],
        out_specs=pl.BlockSpec((MOE_GROUP, D), lambda g, f, ge, gx, ns: (g, 0)),
        scratch_shapes=[pltpu.VMEM((D, MOE_TF), BF16), pltpu.VMEM((D, MOE_TF), BF16),
                        pltpu.VMEM((MOE_TF, D), BF16)],
    )
    return pl.pallas_call(
        kern,
        grid_spec=grid_spec,
        out_shape=jax.ShapeDtypeStruct((P, D), F32),
        compiler_params=_cparams(("arbitrary", "arbitrary")),
        name="moe_experts",
    )(grp_e, grp_x, grp_ns, xg, w_gu, w_gu, w_down,
      b_gu.reshape(L, E, 1, F2), b_gu.reshape(L, E, 1, F2), b_down.reshape(L, E, 1, D), slot_gate)


def _moe_plan(top_idx, gates):
    N = top_idx.shape[0]
    NK = N * TOP_K
    G = NK // MOE_GROUP + N_EXPERTS
    P = G * MOE_GROUP
    eid = top_idx.reshape(NK)
    onehot = (eid[:, None] == jnp.arange(N_EXPERTS, dtype=jnp.int32)[None, :]).astype(jnp.int32)
    csum = jnp.cumsum(onehot, axis=0)
    rank = jnp.take_along_axis(csum, eid[:, None], axis=1)[:, 0] - 1
    counts = csum[-1]
    ngrp = (counts + MOE_GROUP - 1) // MOE_GROUP
    gend = jnp.cumsum(ngrp)
    gstart = gend - ngrp
    pos = gstart[eid] * MOE_GROUP + rank
    tok = jnp.arange(NK, dtype=jnp.int32) // TOP_K
    slot_tok = jnp.zeros((P,), jnp.int32).at[pos].set(tok)
    slot_gate = jnp.zeros((P,), F32).at[pos].set(gates.reshape(NK))
    gidx = jnp.arange(G, dtype=jnp.int32)
    n_groups = gend[-1]
    last = jnp.maximum(n_groups - 1, 0)
    gclamp = jnp.minimum(gidx, last)
    grp_e = jnp.minimum(jnp.searchsorted(gend, gclamp, side="right"), N_EXPERTS - 1).astype(jnp.int32)
    rows = jnp.clip(counts[grp_e] - (gidx - gstart[grp_e]) * MOE_GROUP, 0, MOE_GROUP)
    rows = jnp.where(gidx < n_groups, rows, 0)
    grp_ns = ((rows + MOE_SUB - 1) // MOE_SUB).astype(jnp.int32)
    return pos.reshape(N, TOP_K), slot_tok, slot_gate.reshape(P, 1), grp_e, gclamp.astype(jnp.int32), grp_ns


def _moe(layer, h, top_idx, gates, w_gu, b_gu, w_down, b_down):
    pos, slot_tok, slot_gate, grp_e, grp_x, grp_ns = _moe_plan(top_idx, gates)
    xg = jnp.take(h, slot_tok, axis=0)
    ys = _moe_experts(layer, xg, slot_gate, grp_e, grp_x, grp_ns, w_gu, b_gu, w_down, b_down)
    N, D = h.shape
    return jnp.take(ys, pos.reshape(-1), axis=0).reshape(N, TOP_K, D).sum(axis=1)


def _residual_kernel(x_ref, y_ref, g_ref, o_ref):
    o_ref[...] = x_ref[...] + g_ref[...] * y_ref[...]


def _residual(x, y, gate, tm=512):
    S, D = x.shape
    row = pl.BlockSpec((tm, D), lambda i: (i, 0))
    return pl.pallas_call(
        _residual_kernel,
        grid=(S // tm,),
        in_specs=[row, row, pl.BlockSpec((1, D), lambda i: (0, 0))],
        out_specs=row,
        out_shape=jax.ShapeDtypeStruct((S, D), F32),
        compiler_params=_cparams(("arbitrary",)),
        name="residual",
    )(x, y, gate.reshape(1, D))


def _mixer(h, w_in, b_mgates, m_norm, lam, a_norm, w_br_m, w_br_a, lam_init, rope):
    w_m = w_in[:, :OFF_MG].astype(BF16)
    w_g = jnp.zeros((w_in.shape[0], LANES), BF16).at[:, :M_NG].set(w_in[:, OFF_MG:OFF_A].astype(BF16))
    w_a = w_in[:, OFF_A:OFF_GATE].astype(BF16)
    w_gate = w_in[:, OFF_GATE:].astype(BF16)

    proj_m = _proj_plain(h, w_m, BF16)
    gates = _proj_plain(h, w_g, F32)[:, :M_NG] + b_mgates[None, :]
    proj_a = _proj_attn(h, w_a, *rope)
    proj_g = _proj_plain(h, w_gate, BF16)

    h2 = _mlstm(proj_m, gates)
    y_m = _mlstm_out(h2, proj_m, m_norm)
    y_a = _attention(proj_a, lam, a_norm, lam_init)
    return _merge(y_m, y_a, w_br_m.astype(BF16), w_br_a.astype(BF16), proj_g)


def kernel(x, c, norm_mix, norm_ffn, w_ada, b_ada, w_in, b_mgates, m_norm, lam_q1, lam_k1, lam_q2, lam_k2,
           a_norm, w_br_m, w_br_a, w_out, w_router, b_router, w_gu, b_gu, w_down, b_down, norm_final):
    B, S, D = x.shape
    assert B == 1 and D == D_MODEL
    xs = x.reshape(S, D)
    rope = _rope_lane_tables(S)
    mod = _ada(c, w_ada, b_ada)
    for l in range(DEPTH):
        lam_init = 0.8 - 0.6 * float(np.exp(-0.3 * l))
        sh_m, sc_m, g_m, sh_f, sc_f, g_f = jnp.split(mod[l], 6)
        lam = (jnp.exp(jnp.sum(lam_q1[l] * lam_k1[l])) - jnp.exp(jnp.sum(lam_q2[l] * lam_k2[l])) + lam_init)
        h = _prenorm(xs, norm_mix[l], sc_m, sh_m)
        merged = _mixer(h, w_in[l], b_mgates[l], m_norm[l], lam, a_norm[l], w_br_m[l], w_br_a[l], lam_init, rope)
        xs = _outproj_residual(merged, w_out[l].astype(BF16), xs, g_m)
        h, top_idx, gates = _ffn_prenorm(xs, norm_ffn[l], sc_f, sh_f, w_router[l], b_router[l])
        y = _moe(l, h, top_idx, gates, w_gu, b_gu, w_down, b_down)
        xs = _residual(xs, y, g_f)
    return _final_norm(xs, norm_final).reshape(B, S, D)
```
